```python
import math
import jax, jax.numpy as jnp
from jax import lax
import numpy as np

D_MODEL = 1024
BATCH = 16
SEQ = 2048
DEPTH = 1

D_MIX = D_MODEL
D_SSM = D_MIX // 2
D_POOL = D_MIX - D_SSM
SSM_HEAD_DIM = 64
SSM_HEADS = D_SSM // SSM_HEAD_DIM
SSM_GROUPS = 2
SSM_STATE = 128
SSM_CONV = 4
SSM_CHUNK = 128
D_XBC = D_SSM + 2 * SSM_GROUPS * SSM_STATE
DT_MIN = 1e-3
DT_MAX = 1e-1
POOL_WINDOWS = (2, 4, 8, 16)
N_POOL_GROUPS = len(POOL_WINDOWS)
POOL_GROUP = D_POOL // N_POOL_GROUPS
D_IN_PROJ = D_SSM + D_XBC + SSM_HEADS + D_POOL
N_EXPERT_GROUPS = 4
EXPERTS_PER_GROUP = 4
N_EXPERTS = N_EXPERT_GROUPS * EXPERTS_PER_GROUP
TOP_K_INNER = 2
D_FF_EXPERT = 512
LN_EPS = 1e-5
RMS_EPS = 1e-5
DEEPNORM_ALPHA = (2.0 * DEPTH) ** 0.25
DEEPNORM_BETA = (8.0 * DEPTH) ** -0.25

kernel_name = "hybrid_ssd_pool_hmoe_deepnorm"


def layer_norm(x, g, b):
    xf = x.astype(jnp.float32)
    mu = jnp.mean(xf, axis=-1, keepdims=True)
    var = jnp.mean(jnp.square(xf - mu), axis=-1, keepdims=True)
    y = (xf - mu) * lax.rsqrt(var + LN_EPS)
    return (y * g.astype(jnp.float32) + b.astype(jnp.float32)).astype(x.dtype)


def causal_depthwise_conv(u, w, b):
    k_width = w.shape[0]
    seq = u.shape[1]
    up = jnp.pad(u, ((0, 0), (k_width - 1, 0), (0, 0)))
    out = b
    for k in range(k_width):
        out = out + up[:, k:k + seq, :] * w[k]
    return out


def segsum_decay(a):
    t = a.shape[-1]
    ae = jnp.broadcast_to(a[..., :, None], a.shape + (t,))
    strict = jnp.tril(jnp.ones((t, t), dtype=bool), -1)
    cs = jnp.cumsum(jnp.where(strict, ae, 0.0), axis=-2)
    lower = jnp.tril(jnp.ones((t, t), dtype=bool), 0)
    return jnp.where(lower, jnp.exp(cs), 0.0)


def ssd_chunked(xh, dt, a_head, bm, cm):
    bsz, seq, nh, hp = xh.shape
    ng, ns = bm.shape[2], bm.shape[3]
    nj = nh // ng
    q = SSM_CHUNK
    nc = seq // q
    x = (xh * dt[..., None]).reshape(bsz, nc, q, ng, nj, hp)
    a = (dt * a_head).reshape(bsz, nc, q, ng, nj).transpose(0, 3, 4, 1, 2)
    bc = bm.reshape(bsz, nc, q, ng, ns)
    cc = cm.reshape(bsz, nc, q, ng, ns)
    a_cum = jnp.cumsum(a, axis=-1)
    decay = segsum_decay(a)
    scores = jnp.einsum("bclgn,bcsgn->bgcls", cc, bc)
    y_diag = jnp.einsum("bgcls,bgjcls,bcsgjp->bclgjp", scores, decay, x)
    decay_states = jnp.exp(a_cum[..., -1:] - a_cum)
    states = jnp.einsum("bclgn,bgjcl,bclgjp->bcgjpn", bc, decay_states, x)
    chunk_decay = jnp.exp(a_cum[..., -1])

    def step(h, inp):
        s, d = inp
        return h * d[..., None, None] + s, h

    h0 = jnp.zeros((bsz, ng, nj, hp, ns), dtype=states.dtype)
    _, prev = lax.scan(step, h0, (jnp.moveaxis(states, 1, 0), jnp.moveaxis(chunk_decay, -1, 0)))
    prev = jnp.moveaxis(prev, 0, 1)
    y_off = jnp.einsum("bclgn,bcgjpn,bgjcl->bclgjp", cc, prev, jnp.exp(a_cum))
    return (y_diag + y_off).reshape(bsz, seq, nh, hp)


def multiscale_pool(u, w_pool, b_pool, scale):
    bsz, seq, _ = u.shape
    uf = u.astype(jnp.float32)
    cs = jnp.cumsum(uf, axis=1)
    pos = jnp.arange(seq, dtype=jnp.float32) + 1.0
    outs = []
    for i, w in enumerate(POOL_WINDOWS):
        sl = slice(i * POOL_GROUP, (i + 1) * POOL_GROUP)
        csg = cs[..., sl]
        lag = jnp.pad(csg[:, :seq - w], ((0, 0), (w, 0), (0, 0)))
        cnt = jnp.minimum(pos, float(w))[:, None]
        outs.append((csg - lag) / cnt - uf[..., sl])
    p = jnp.stack(outs, axis=2)
    mixed = jnp.einsum("blgc,gcd->blgd", p, w_pool.astype(jnp.float32)) + b_pool.astype(jnp.float32)
    return (mixed.reshape(bsz, seq, D_POOL) * scale.astype(jnp.float32)).astype(u.dtype)


def token_mixer(x, w_in, conv_w, conv_b, dt_bias, a_log, d_skip, ssm_norm_g,
                w_pool, b_pool, pool_scale, w_out):
    bsz, seq, _ = x.shape
    proj = x @ w_in
    s1, s2, s3 = D_SSM, D_SSM + D_XBC, D_SSM + D_XBC + SSM_HEADS
    z, xbc, dt_raw, u = proj[..., :s1], proj[..., s1:s2], proj[..., s2:s3], proj[..., s3:]
    xbc = jax.nn.silu(causal_depthwise_conv(xbc, conv_w, conv_b)).astype(jnp.float32)
    nbc = SSM_GROUPS * SSM_STATE
    xs = xbc[..., :D_SSM].reshape(bsz, seq, SSM_HEADS, SSM_HEAD_DIM)
    bm = xbc[..., D_SSM:D_SSM + nbc].reshape(bsz, seq, SSM_GROUPS, SSM_STATE)
    cm = xbc[..., D_SSM + nbc:].reshape(bsz, seq, SSM_GROUPS, SSM_STATE)
    dt = jax.nn.softplus(dt_raw.astype(jnp.float32) + dt_bias.astype(jnp.float32))
    a_head = -jnp.exp(a_log.astype(jnp.float32))
    y = ssd_chunked(xs, dt, a_head, bm, cm) + xs * d_skip.astype(jnp.float32)[:, None]
    y = y.reshape(bsz, seq, D_SSM) * jax.nn.silu(z.astype(jnp.float32))
    yg = y.reshape(bsz, seq, SSM_GROUPS, D_SSM // SSM_GROUPS)
    yg = yg * lax.rsqrt(jnp.mean(jnp.square(yg), axis=-1, keepdims=True) + RMS_EPS)
    y_ssd = (yg.reshape(bsz, seq, D_SSM) * ssm_norm_g.astype(jnp.float32)).astype(x.dtype)
    y_pool = multiscale_pool(u, w_pool, b_pool, pool_scale)
    return jnp.concatenate([y_ssd, y_pool], axis=-1) @ w_out


def hierarchical_moe(x, w_router_group, b_router_group, w_router_expert, b_router_expert,
                     w_gate, w_up, w_down):
    bsz, seq, d = x.shape
    xt = x.reshape(-1, d)
    xf = xt.astype(jnp.float32)
    g_logits = xf @ w_router_group.astype(jnp.float32) + b_router_group.astype(jnp.float32)
    g_prob = jax.nn.softmax(g_logits, axis=-1)
    g_w, g_idx = lax.top_k(g_prob, 1)
    e_logits = (xf @ w_router_expert.astype(jnp.float32) + b_router_expert.astype(jnp.float32))
    e_logits = e_logits.reshape(-1, N_EXPERT_GROUPS, EXPERTS_PER_GROUP)
    e_sel = jnp.take_along_axis(e_logits, g_idx[:, :, None], axis=1)[:, 0]
    top_v, top_i = lax.top_k(e_sel, TOP_K_INNER)
    comb = g_w * jax.nn.softmax(top_v, axis=-1)
    eid = g_idx * EXPERTS_PER_GROUP + top_i
    dense_w = jnp.sum(jax.nn.one_hot(eid, N_EXPERTS, dtype=jnp.float32) * comb[..., None], axis=1)
    dense_w = dense_w.astype(xt.dtype)
    y = jnp.zeros_like(xt)
    for e in range(N_EXPERTS):
        h = jax.nn.silu(xt @ w_gate[e]) * (xt @ w_up[e])
        y = y + dense_w[:, e:e + 1] * (h @ w_down[e])
    return y.reshape(bsz, seq, d)


def setup_inputs(seed: int = 0) -> dict:
    key = jax.random.key(seed)
    ks = jax.random.split(key, 32)
    f32 = jnp.float32

    def nrm(k, shape, scale):
        return jax.random.normal(k, shape, f32) * scale

    x = nrm(ks[0], (BATCH, SEQ, D_MODEL), 1.0)
    ln0_g = 1.0 + nrm(ks[1], (D_MODEL,), 0.02)
    ln0_b = nrm(ks[2], (D_MODEL,), 0.02)
    w_in = nrm(ks[3], (DEPTH, D_MODEL, D_IN_PROJ), D_MODEL ** -0.5)
    conv_w = nrm(ks[4], (DEPTH, SSM_CONV, D_XBC), SSM_CONV ** -0.5)
    conv_b = nrm(ks[5], (DEPTH, D_XBC), 0.02)
    uu = jax.random.uniform(ks[6], (DEPTH, SSM_HEADS), f32)
    dt0 = jnp.exp(uu * (math.log(DT_MAX) - math.log(DT_MIN)) + math.log(DT_MIN))
    dt_bias = dt0 + jnp.log(-jnp.expm1(-dt0))
    a_log = jnp.log(jax.random.uniform(ks[7], (DEPTH, SSM_HEADS), f32, 1.0, 16.0))
    d_skip = 1.0 + nrm(ks[8], (DEPTH, SSM_HEADS), 0.02)
    ssm_norm_g = 1.0 + nrm(ks[9], (DEPTH, D_SSM), 0.02)
    w_pool = nrm(ks[10], (DEPTH, N_POOL_GROUPS, POOL_GROUP, POOL_GROUP), POOL_GROUP ** -0.5)
    b_pool = nrm(ks[11], (DEPTH, N_POOL_GROUPS, POOL_GROUP), 0.02)
    pool_scale = 1.0 + nrm(ks[12], (DEPTH, D_POOL), 0.02)
    w_out = nrm(ks[13], (DEPTH, D_MIX, D_MODEL), D_MIX ** -0.5 * DEEPNORM_BETA)
    ln1_g = 1.0 + nrm(ks[14], (DEPTH, D_MODEL), 0.02)
    ln1_b = nrm(ks[15], (DEPTH, D_MODEL), 0.02)
    w_router_group = nrm(ks[16], (DEPTH, D_MODEL, N_EXPERT_GROUPS), D_MODEL ** -0.5)
    b_router_group = nrm(ks[17], (DEPTH, N_EXPERT_GROUPS), 0.01)
    w_router_expert = nrm(ks[18], (DEPTH, D_MODEL, N_EXPERTS), D_MODEL ** -0.5)
    b_router_expert = nrm(ks[19], (DEPTH, N_EXPERTS), 0.01)
    w_gate = nrm(ks[20], (DEPTH, N_EXPERTS, D_MODEL, D_FF_EXPERT), D_MODEL ** -0.5)
    w_up = nrm(ks[21], (DEPTH, N_EXPERTS, D_MODEL, D_FF_EXPERT), D_MODEL ** -0.5)
    w_down = nrm(ks[22], (DEPTH, N_EXPERTS, D_FF_EXPERT, D_MODEL), D_FF_EXPERT ** -0.5 * DEEPNORM_BETA)
    ln2_g = 1.0 + nrm(ks[23], (DEPTH, D_MODEL), 0.02)
    ln2_b = nrm(ks[24], (DEPTH, D_MODEL), 0.02)
    return {"x": x, "ln0_g": ln0_g, "ln0_b": ln0_b, "w_in": w_in, "conv_w": conv_w,
            "conv_b": conv_b, "dt_bias": dt_bias, "a_log": a_log, "d_skip": d_skip,
            "ssm_norm_g": ssm_norm_g, "w_pool": w_pool, "b_pool": b_pool,
            "pool_scale": pool_scale, "w_out": w_out, "ln1_g": ln1_g, "ln1_b": ln1_b,
            "w_router_group": w_router_group, "b_router_group": b_router_group,
            "w_router_expert": w_router_expert, "b_router_expert": b_router_expert,
            "w_gate": w_gate, "w_up": w_up, "w_down": w_down, "ln2_g": ln2_g, "ln2_b": ln2_b}


def reference(x, ln0_g, ln0_b, w_in, conv_w, conv_b, dt_bias, a_log, d_skip, ssm_norm_g,
              w_pool, b_pool, pool_scale, w_out, ln1_g, ln1_b, w_router_group, b_router_group,
              w_router_expert, b_router_expert, w_gate, w_up, w_down, ln2_g, ln2_b):
    x = layer_norm(x, ln0_g, ln0_b)
    for layer in range(DEPTH):
        mix = token_mixer(x, w_in[layer], conv_w[layer], conv_b[layer], dt_bias[layer],
                          a_log[layer], d_skip[layer], ssm_norm_g[layer], w_pool[layer],
                          b_pool[layer], pool_scale[layer], w_out[layer])
        x = layer_norm(DEEPNORM_ALPHA * x + mix, ln1_g[layer], ln1_b[layer])
        ffn = hierarchical_moe(x, w_router_group[layer], b_router_group[layer],
                               w_router_expert[layer], b_router_expert[layer],
                               w_gate[layer], w_up[layer], w_down[layer])
        x = layer_norm(DEEPNORM_ALPHA * x + ffn, ln2_g[layer], ln2_b[layer])
    return x
```

```python
import functools
import math

import jax
import jax.numpy as jnp
from jax import lax
from jax.experimental import pallas as pl
from jax.experimental.pallas import tpu as pltpu

F32 = jnp.float32
BF16 = jnp.bfloat16

D_MODEL = 1024
D_SSM = 512
D_POOL = 512
SSM_HEAD_DIM = 64
SSM_HEADS = 8
SSM_GROUPS = 2
SSM_STATE = 128
SSM_CONV = 4
SSM_CHUNK = 128
D_XBC = 1024
POOL_WINDOWS = (2, 4, 8, 16)
POOL_GROUP = 128
N_EXPERT_GROUPS = 4
EXPERTS_PER_GROUP = 4
N_EXPERTS = 16
D_FF = 512
LN_EPS = 1e-5
RMS_EPS = 1e-5
ALPHA = 2.0 ** 0.25

LANES = 128
HALO = 16

C_Z = 0
C_XBC = D_SSM
C_U = D_SSM + D_XBC
C_DT = C_U + D_POOL
C_ALL = C_DT + LANES

R_GROUP = 0
R_EXPERT = 8

PAIR_A = (0, 0, 0, 1, 1, 2)
PAIR_B = (1, 2, 3, 2, 3, 3)
N_PAIRS = 6
N_CLASSES = N_EXPERT_GROUPS * N_PAIRS

X1E_W = D_MODEL + LANES

TL = 512
TM = 256


def _silu(v):
    return v * (1.0 / (1.0 + jnp.exp(-v)))


def _layer_norm(v, g, b):
    mu = jnp.mean(v, axis=-1, keepdims=True)
    c = v - mu
    var = jnp.mean(c * c, axis=-1, keepdims=True)
    return c * lax.rsqrt(var + LN_EPS) * g + b


def _split3(a):
    hi = a.astype(BF16)
    r = a - hi.astype(F32)
    mid = r.astype(BF16)
    lo = (r - mid.astype(F32)).astype(BF16)
    return hi, mid, lo


def _dot(a, b):
    return jnp.dot(a, b, preferred_element_type=F32)


def _pair_expand(col_fn, lane_lo):
    blocks = []
    for k in range(SSM_HEADS // 2):
        blocks.append(jnp.where(lane_lo, col_fn(2 * k), col_fn(2 * k + 1)))
    return jnp.concatenate(blocks, axis=1)


def _mixer_kernel(x_ref, ln0g_ref, ln0b_ref, wall_ref, convw_ref, convb_ref, dtb_ref,
                  alog_ref, dskip_ref, normg_ref, wpool_ref, bpool_ref, pscale_ref,
                  wout_ref, ln1g_ref, ln1b_ref, wr_ref, br_ref,
                  x1e_ref, cls_ref,
                  xn_s, xnb_s, proj_s, ymix_s, mix_s, hstate):
    s_idx = pl.program_id(1)
    n_blk = TL // SSM_CHUNK

    @pl.when(s_idx == 0)
    def _():
        hstate[...] = jnp.zeros_like(hstate)
        proj_s[0:HALO, :] = jnp.zeros((HALO, C_ALL), F32)

    @pl.when(s_idx != 0)
    def _():
        proj_s[0:HALO, :] = proj_s[TL:TL + HALO, :]

    def ln0_blk(i, carry):
        r0 = pl.multiple_of(i * SSM_CHUNK, SSM_CHUNK)
        xn = _layer_norm(x_ref[pl.ds(r0, SSM_CHUNK), :], ln0g_ref[...], ln0b_ref[...])
        xn_s[pl.ds(r0, SSM_CHUNK), :] = xn
        xnb_s[pl.ds(r0, SSM_CHUNK), :] = xn.astype(BF16)
        return carry
    lax.fori_loop(0, n_blk, ln0_blk, 0)

    proj_s[HALO:HALO + TL, :] = _dot(xnb_s[...], wall_ref[...])

    row_i = lax.broadcasted_iota(jnp.int32, (SSM_CHUNK, LANES), 0)
    lane_i = lax.broadcasted_iota(jnp.int32, (SSM_CHUNK, LANES), 1)
    tril = row_i >= lane_i
    tri = jnp.where(tril, 1.0, 0.0).astype(BF16)
    tri_t = jnp.where(row_i <= lane_i, 1.0, 0.0).astype(BF16)
    lane_lo = lane_i < SSM_HEAD_DIM
    lane_lo1 = lane_lo[0:1, :]
    a_head = -jnp.exp(alog_ref[...])

    def chunk(i, carry):
        r0 = pl.multiple_of(i * SSM_CHUNK, SSM_CHUNK)
        p0 = r0 + HALO

        ext = proj_s[pl.ds(pl.multiple_of(p0 - 8, 8), SSM_CHUNK + 8), C_XBC:C_XBC + D_XBC]
        cw = convw_ref[...]
        acc = ext * cw[3:4, :]
        for k in range(3):
            acc = acc + pltpu.roll(ext, 3 - k, 0) * cw[k:k + 1, :]
        xbc = _silu(acc[8:, :] + convb_ref[...])
        xs = xbc[:, 0:D_SSM]
        bm = xbc[:, D_SSM:D_SSM + 2 * SSM_STATE]
        cm = xbc[:, D_SSM + 2 * SSM_STATE:]

        dt_raw = proj_s[pl.ds(p0, SSM_CHUNK), C_DT:C_DT + LANES] + dtb_ref[...]
        dt = jnp.maximum(dt_raw, 0.0) + jnp.log1p(jnp.exp(-jnp.abs(dt_raw)))
        a = dt * a_head
        a_hi, a_mid, a_lo = _split3(a)
        acum = _dot(tri, a_lo) + _dot(tri, a_mid) + _dot(tri, a_hi)
        dt_t = dt.T
        a_t = a.T[0:16, :]
        t_hi, t_mid, t_lo = _split3(a_t)
        acum_t = _dot(t_lo, tri_t) + _dot(t_mid, tri_t) + _dot(t_hi, tri_t)
        tot = acum[SSM_CHUNK - 1:SSM_CHUNK, :]
        wdec = dt * jnp.exp(tot - acum)
        eac = jnp.exp(acum)

        w_e = _pair_expand(
            lambda h: jnp.broadcast_to(wdec[:, h:h + 1], (SSM_CHUNK, LANES)), lane_lo)
        xw = (xs * w_e).astype(BF16)
        etot = jnp.exp(tot)
        etot_e = _pair_expand(
            lambda h: jnp.broadcast_to(etot[:, h:h + 1], (1, LANES)), lane_lo1)

        ys = []
        for g in range(SSM_GROUPS):
            bg = bm[:, g * SSM_STATE:(g + 1) * SSM_STATE]
            cg = cm[:, g * SSM_STATE:(g + 1) * SSM_STATE]
            bt_b = bg.T.astype(BF16)
            scores = _dot(cg.astype(BF16), bt_b)
            gsl = slice(g * 256, (g + 1) * 256)
            st = _dot(bt_b, xw[:, gsl])
            for kk in range(2):
                k = 2 * g + kk
                psl = slice(k * LANES, (k + 1) * LANES)
                rcat = jnp.concatenate([xs[:, psl], hstate[:, psl]], axis=0)
                y_pair = None
                for hh in range(2):
                    h = 2 * k + hh
                    colb = jnp.broadcast_to(acum[:, h:h + 1], (SSM_CHUNK, LANES))
                    diff = colb - acum_t[h:h + 1, :]
                    decay = jnp.exp(jnp.where(tril, diff, -jnp.inf))
                    m_h = scores * decay * dt_t[h:h + 1, :]
                    c_off = cg * jnp.broadcast_to(eac[:, h:h + 1], (SSM_CHUNK, LANES))
                    lhs = jnp.concatenate([m_h, c_off], axis=1).astype(BF16)
                    keep = lane_lo if hh == 0 else jnp.logical_not(lane_lo)
                    keep2 = jnp.concatenate([keep, keep], axis=0)
                    rhs = jnp.where(keep2, rcat, 0.0).astype(BF16)
                    part = _dot(lhs, rhs)
                    y_pair = part if y_pair is None else y_pair + part
                ys.append(y_pair)
            hstate[:, gsl] = hstate[:, gsl] * etot_e[:, gsl] + st
        y = jnp.concatenate(ys, axis=1)
        y = y + xs * dskip_ref[...]
        z = proj_s[pl.ds(p0, SSM_CHUNK), C_Z:C_Z + D_SSM]
        y = y * _silu(z)
        half = D_SSM // SSM_GROUPS
        outs = []
        for g in range(SSM_GROUPS):
            yg = y[:, g * half:(g + 1) * half]
            ms = jnp.mean(yg * yg, axis=-1, keepdims=True)
            outs.append(yg * lax.rsqrt(ms + RMS_EPS))
        y_ssd = jnp.concatenate(outs, axis=1) * normg_ref[...]
        ymix_s[pl.ds(r0, SSM_CHUNK), 0:D_SSM] = y_ssd.astype(BF16)

        uext = proj_s[pl.ds(p0 - HALO, SSM_CHUNK + HALO), C_U:C_U + D_POOL]
        pos = (s_idx * TL + r0 + 1 + row_i).astype(F32)
        for gi, w in enumerate(POOL_WINDOWS):
            ug = uext[:, gi * POOL_GROUP:(gi + 1) * POOL_GROUP]
            sw = ug
            step = 1
            while step < w:
                sw = sw + pltpu.roll(sw, step, 0)
                step *= 2
            cnt = jnp.minimum(pos, float(w))
            p = sw[HALO:, :] / cnt - ug[HALO:, :]
            mixed = _dot(p.astype(BF16), wpool_ref[gi]) + bpool_ref[:, gi * POOL_GROUP:(gi + 1) * POOL_GROUP]
            mixed = mixed * pscale_ref[:, gi * POOL_GROUP:(gi + 1) * POOL_GROUP]
            ymix_s[pl.ds(r0, SSM_CHUNK), D_SSM + gi * POOL_GROUP:D_SSM + (gi + 1) * POOL_GROUP] = (
                mixed.astype(BF16))
        return carry
    lax.fori_loop(0, n_blk, chunk, 0)

    mix_s[...] = _dot(ymix_s[...], wout_ref[...])

    sub_i = lax.broadcasted_iota(jnp.int32, (8, LANES), 0)

    def tail_blk(i, carry):
        r0 = pl.multiple_of(i * SSM_CHUNK, SSM_CHUNK)
        v = ALPHA * xn_s[pl.ds(r0, SSM_CHUNK), :] + mix_s[pl.ds(r0, SSM_CHUNK), :]
        x1 = _layer_norm(v, ln1g_ref[...], ln1b_ref[...])
        x1e_ref[pl.ds(r0, SSM_CHUNK), 0:D_MODEL] = x1
        logits = _dot(x1.astype(BF16), wr_ref[...]) + br_ref[...]
        lt = logits.T
        gl = [lt[R_GROUP + k:R_GROUP + k + 1, :] for k in range(N_EXPERT_GROUPS)]
        gmax = jnp.maximum(jnp.maximum(gl[0], gl[1]), jnp.maximum(gl[2], gl[3]))
        ge = [jnp.exp(v_ - gmax) for v_ in gl]
        gsum = ge[0] + ge[1] + ge[2] + ge[3]
        gp = [e_ / gsum for e_ in ge]
        pmax = jnp.maximum(jnp.maximum(gp[0], gp[1]), jnp.maximum(gp[2], gp[3]))
        gidx = jnp.where(gp[0] == pmax, 0, jnp.where(gp[1] == pmax, 1, jnp.where(gp[2] == pmax, 2, 3)))
        es = []
        for j in range(EXPERTS_PER_GROUP):
            rows = [lt[R_EXPERT + 4 * k + j:R_EXPERT + 4 * k + j + 1, :] for k in range(N_EXPERT_GROUPS)]
            es.append(jnp.where(gidx == 0, rows[0], jnp.where(gidx == 1, rows[1],
                      jnp.where(gidx == 2, rows[2], rows[3]))))
        v1 = jnp.maximum(jnp.maximum(es[0], es[1]), jnp.maximum(es[2], es[3]))
        i1 = jnp.where(es[0] == v1, 0, jnp.where(es[1] == v1, 1, jnp.where(es[2] == v1, 2, 3)))
        em = [jnp.where(i1 == j, -jnp.inf, es[j]) for j in range(EXPERTS_PER_GROUP)]
        v2 = jnp.maximum(jnp.maximum(em[0], em[1]), jnp.maximum(em[2], em[3]))
        i2 = jnp.where(em[0] == v2, 0, jnp.where(em[1] == v2, 1, jnp.where(em[2] == v2, 2, 3)))
        e2 = jnp.exp(v2 - v1)
        den = 1.0 + e2
        c1 = pmax * (1.0 / den)
        c2 = pmax * (e2 / den)
        first_low = i1 < i2
        ia = jnp.where(first_low, i1, i2)
        ib = jnp.where(first_low, i2, i1)
        wa = jnp.where(first_low, c1, c2)
        wb = jnp.where(first_low, c2, c1)
        pid = jnp.where(ia == 0, ib - 1, jnp.where(ia == 1, ib + 1, 5))
        cls = gidx * N_PAIRS + pid
        cls_ref[:, pl.ds(r0, SSM_CHUNK)] = jnp.where(sub_i == 0, cls, 0)
        wrows = jnp.where(row_i == 0, wa, jnp.where(row_i == 1, wb, 0.0))
        x1e_ref[pl.ds(r0, SSM_CHUNK), D_MODEL:X1E_W] = wrows.T
        return carry
    lax.fori_loop(0, n_blk, tail_blk, 0)


def _mixer_call(x, ln0_g, ln0_b, w_all, conv_w, conv_b, dtb, alog, dskip, normg, w_pool,
                b_pool, pscale, w_out, ln1_g, ln1_b, w_r, b_r):
    bsz, seq, _ = x.shape
    n_tok = bsz * seq
    n_s = seq // TL

    def full(shape):
        return pl.BlockSpec(shape, lambda b, s: (0,) * len(shape))

    in_specs = [
        pl.BlockSpec((None, TL, D_MODEL), lambda b, s: (b, s, 0)),
        full((1, D_MODEL)), full((1, D_MODEL)),
        full((D_MODEL, C_ALL)),
        full((SSM_CONV, D_XBC)), full((1, D_XBC)),
        full((1, LANES)), full((1, LANES)),
        full((1, D_SSM)), full((1, D_SSM)),
        full((len(POOL_WINDOWS), POOL_GROUP, POOL_GROUP)), full((1, D_POOL)), full((1, D_POOL)),
        full((D_MODEL, D_MODEL)),
        full((1, D_MODEL)), full((1, D_MODEL)),
        full((D_MODEL, LANES)), full((1, LANES)),
    ]
    out_specs = [
        pl.BlockSpec((TL, X1E_W), lambda b, s: (b * n_s + s, 0)),
        pl.BlockSpec((8, TL), lambda b, s: (0, b * n_s + s)),
    ]
    out_shape = [
        jax.ShapeDtypeStruct((n_tok, X1E_W), F32),
        jax.ShapeDtypeStruct((8, n_tok), jnp.int32),
    ]
    scratch = [
        pltpu.VMEM((TL, D_MODEL), F32),
        pltpu.VMEM((TL, D_MODEL), BF16),
        pltpu.VMEM((TL + HALO, C_ALL), F32),
        pltpu.VMEM((TL, D_MODEL), BF16),
        pltpu.VMEM((TL, D_MODEL), F32),
        pltpu.VMEM((SSM_STATE, D_SSM), F32),
    ]
    return pl.pallas_call(
        _mixer_kernel,
        grid=(bsz, n_s),
        in_specs=in_specs,
        out_specs=out_specs,
        out_shape=out_shape,
        scratch_shapes=scratch,
        compiler_params=pltpu.CompilerParams(
            dimension_semantics=("arbitrary", "arbitrary"),
            vmem_limit_bytes=56 * 1024 * 1024),
        name="token_mixer",
    )(x, ln0_g, ln0_b, w_all, conv_w, conv_b, dtb, alog, dskip, normg, w_pool, b_pool,
      pscale, w_out, ln1_g, ln1_b, w_r, b_r)


def _moe_kernel(sorted_ref, ea_ref, eb_ref, start_ref, nvalid_ref,
                x1e_hbm, wga_ref, wua_ref, wda_ref, wgb_ref, wub_ref, wdb_ref,
                ln2g_ref, ln2b_ref,
                out_hbm,
                xbuf, obuf, gsem, ssem, *, n_tok):
    i = pl.program_id(0)
    nv = nvalid_ref[i]

    def gather_copy(tok, r):
        return pltpu.make_async_copy(x1e_hbm.at[pl.ds(tok, 1), :], xbuf.at[pl.ds(r, 1), :], gsem)

    def scatter_copy(tok, r):
        return pltpu.make_async_copy(obuf.at[pl.ds(r, 1), :], out_hbm.at[pl.ds(tok, 1), :], ssem)

    @pl.when(nv > 0)
    def _():
        base = start_ref[i]

        def issue(r, carry):
            tok = sorted_ref[jnp.minimum(base + r, n_tok - 1)]
            gather_copy(tok, r).start()
            return carry
        lax.fori_loop(0, TM, issue, 0)

        def gwait(r, carry):
            gather_copy(0, r).wait()
            return carry
        lax.fori_loop(0, TM, gwait, 0)

        x1 = xbuf[:, 0:D_MODEL]
        xb = x1.astype(BF16)
        wa = xbuf[:, D_MODEL:D_MODEL + 1]
        wb = xbuf[:, D_MODEL + 1:D_MODEL + 2]
        ha = _silu(_dot(xb, wga_ref[...])) * _dot(xb, wua_ref[...]) * wa
        hb = _silu(_dot(xb, wgb_ref[...])) * _dot(xb, wub_ref[...]) * wb
        y = _dot(ha.astype(BF16), wda_ref[...]) + _dot(hb.astype(BF16), wdb_ref[...])
        obuf[...] = _layer_norm(ALPHA * x1 + y, ln2g_ref[...], ln2b_ref[...])

        def scat(r, carry):
            scatter_copy(sorted_ref[base + r], r).start()
            return carry
        lax.fori_loop(0, nv, scat, 0)

        def swait(r, carry):
            scatter_copy(0, r).wait()
            return carry
        lax.fori_loop(0, nv, swait, 0)


def _moe_call(sorted_tok, tile_ea, tile_eb, tile_start, tile_nvalid, x1e, wg, wu, wd, ln2_g, ln2_b):
    n_tok = x1e.shape[0]
    n_tiles = tile_ea.shape[0]

    def wspec(shape, which):
        if which == 0:
            return pl.BlockSpec((None,) + shape, lambda i, st, ea, eb, s0, nv: (ea[i], 0, 0))
        return pl.BlockSpec((None,) + shape, lambda i, st, ea, eb, s0, nv: (eb[i], 0, 0))

    vec = pl.BlockSpec((1, D_MODEL), lambda i, st, ea, eb, s0, nv: (0, 0))
    grid_spec = pltpu.PrefetchScalarGridSpec(
        num_scalar_prefetch=5,
        grid=(n_tiles,),
        in_specs=[
            pl.BlockSpec(memory_space=pl.ANY),
            wspec((D_MODEL, D_FF), 0), wspec((D_MODEL, D_FF), 0), wspec((D_FF, D_MODEL), 0),
            wspec((D_MODEL, D_FF), 1), wspec((D_MODEL, D_FF), 1), wspec((D_FF, D_MODEL), 1),
            vec, vec,
        ],
        out_specs=pl.BlockSpec(memory_space=pl.ANY),
        scratch_shapes=[
            pltpu.VMEM((TM, X1E_W), F32),
            pltpu.VMEM((TM, D_MODEL), F32),
            pltpu.SemaphoreType.DMA,
            pltpu.SemaphoreType.DMA,
        ],
    )
    return pl.pallas_call(
        functools.partial(_moe_kernel, n_tok=n_tok),
        grid_spec=grid_spec,
        out_shape=jax.ShapeDtypeStruct((n_tok, D_MODEL), F32),
        compiler_params=pltpu.CompilerParams(
            dimension_semantics=("arbitrary",),
            vmem_limit_bytes=48 * 1024 * 1024),
        name="sparse_moe",
    )(sorted_tok, tile_ea, tile_eb, tile_start, tile_nvalid, x1e, wg, wu, wd, wg, wu, wd,
      ln2_g, ln2_b)


def _dispatch_plan(cls, n_tok):
    n_tiles = n_tok // TM + N_CLASSES - 1
    tok = jnp.arange(n_tok, dtype=jnp.int32)
    key = jnp.sort(cls * n_tok + tok)
    sorted_tok = key % n_tok
    counts = jnp.sum((cls[:, None] == jnp.arange(N_CLASSES, dtype=jnp.int32)[None, :]).astype(jnp.int32), axis=0)
    class_start = jnp.cumsum(counts) - counts
    tiles_per = (counts + TM - 1) // TM
    tile_end = jnp.cumsum(tiles_per)
    tile_first = tile_end - tiles_per
    total = tile_end[-1]
    ti = jnp.arange(n_tiles, dtype=jnp.int32)
    tc = jnp.sum((ti[:, None] >= tile_end[None, :]).astype(jnp.int32), axis=1)
    last_c = jnp.sum((total - 1 >= tile_end).astype(jnp.int32))
    tc = jnp.where(ti < total, tc, last_c)
    k = ti - tile_first[tc]
    start = class_start[tc] + k * TM
    nvalid = jnp.where(ti < total, jnp.clip(counts[tc] - k * TM, 0, TM), 0)
    start = jnp.where(ti < total, start, 0)
    grp = tc // N_PAIRS
    pid = tc % N_PAIRS
    pa = jnp.asarray(PAIR_A, jnp.int32)
    pb = jnp.asarray(PAIR_B, jnp.int32)
    ea = grp * EXPERTS_PER_GROUP + pa[pid]
    eb = grp * EXPERTS_PER_GROUP + pb[pid]
    return (sorted_tok.astype(jnp.int32), ea.astype(jnp.int32), eb.astype(jnp.int32),
            start.astype(jnp.int32), nvalid.astype(jnp.int32))


def kernel(x, ln0_g, ln0_b, w_in, conv_w, conv_b, dt_bias, a_log, d_skip, ssm_norm_g, w_pool,
           b_pool, pool_scale, w_out, ln1_g, ln1_b, w_router_group, b_router_group,
           w_router_expert, b_router_expert, w_gate, w_up, w_down, ln2_g, ln2_b):
    bsz, seq, d = x.shape
    n_tok = bsz * seq
    layer = 0
    s1, s2, s3 = D_SSM, D_SSM + D_XBC, D_SSM + D_XBC + SSM_HEADS
    wi = w_in[layer]
    w_dt = jnp.pad(wi[:, s2:s3], ((0, 0), (0, LANES - SSM_HEADS)))
    w_all = jnp.concatenate([wi[:, :s2], wi[:, s3:], w_dt], axis=1).astype(BF16)
    dtb = jnp.pad(dt_bias[layer], (0, LANES - SSM_HEADS))[None, :]
    alog = jnp.pad(a_log[layer], (0, LANES - SSM_HEADS))[None, :]
    dskip = jnp.repeat(d_skip[layer], SSM_HEAD_DIM)[None, :]
    w_r = jnp.zeros((d, LANES), F32)
    w_r = w_r.at[:, R_GROUP:R_GROUP + N_EXPERT_GROUPS].set(w_router_group[layer])
    w_r = w_r.at[:, R_EXPERT:R_EXPERT + N_EXPERTS].set(w_router_expert[layer]).astype(BF16)
    b_r = jnp.zeros((LANES,), F32)
    b_r = b_r.at[R_GROUP:R_GROUP + N_EXPERT_GROUPS].set(b_router_group[layer])
    b_r = b_r.at[R_EXPERT:R_EXPERT + N_EXPERTS].set(b_router_expert[layer])[None, :]

    x1e, cls8 = _mixer_call(
        x, ln0_g[None, :], ln0_b[None, :], w_all, conv_w[layer], conv_b[layer][None, :], dtb, alog,
        dskip, ssm_norm_g[layer][None, :], w_pool[layer].astype(BF16),
        b_pool[layer].reshape(1, D_POOL), pool_scale[layer][None, :], w_out[layer].astype(BF16),
        ln1_g[layer][None, :], ln1_b[layer][None, :], w_r, b_r)

    plan = _dispatch_plan(cls8[0], n_tok)
    out = _moe_call(*plan, x1e, w_gate[layer].astype(BF16), w_up[layer].astype(BF16),
                    w_down[layer].astype(BF16), ln2_g[layer][None, :], ln2_b[layer][None, :])
    return out.reshape(bsz, seq, d)
```

```python
import functools
import math

import jax
import jax.numpy as jnp
from jax import lax
from jax.experimental import pallas as pl
from jax.experimental.pallas import tpu as pltpu

F32 = jnp.float32
BF16 = jnp.bfloat16

D_MODEL = 1024
D_SSM = 512
D_POOL = 512
SSM_HEAD_DIM = 64
SSM_HEADS = 8
SSM_GROUPS = 2
SSM_STATE = 128
SSM_CONV = 4
SSM_CHUNK = 128
D_XBC = 1024
POOL_WINDOWS = (2, 4, 8, 16)
POOL_GROUP = 128
N_EXPERT_GROUPS = 4
EXPERTS_PER_GROUP = 4
N_EXPERTS = 16
D_FF = 512
LN_EPS = 1e-5
RMS_EPS = 1e-5
ALPHA = 2.0 ** 0.25

LANES = 128
HALO = 16

C_Z = 0
C_XBC = D_SSM
C_U = D_SSM + D_XBC
C_DT = C_U + D_POOL
C_ALL = C_DT + LANES

R_GROUP = 0
R_EXPERT = 8

PAIR_A = (0, 0, 0, 1, 1, 2)
PAIR_B = (1, 2, 3, 2, 3, 3)
N_PAIRS = 6
N_CLASSES = N_EXPERT_GROUPS * N_PAIRS

X1E_W = D_MODEL + LANES

TL = 512
TM = 256


def _silu(v):
    return v * (1.0 / (1.0 + jnp.exp(-v)))


def _layer_norm(v, g, b):
    mu = jnp.mean(v, axis=-1, keepdims=True)
    c = v - mu
    var = jnp.mean(c * c, axis=-1, keepdims=True)
    return c * lax.rsqrt(var + LN_EPS) * g + b


def _split3(a):
    hi = a.astype(BF16)
    r = a - hi.astype(F32)
    mid = r.astype(BF16)
    lo = (r - mid.astype(F32)).astype(BF16)
    return hi, mid, lo


def _dot(a, b):
    return jnp.dot(a, b, preferred_element_type=F32)


def _pair_expand(col_fn, lane_lo):
    blocks = []
    for k in range(SSM_HEADS // 2):
        blocks.append(jnp.where(lane_lo, col_fn(2 * k), col_fn(2 * k + 1)))
    return jnp.concatenate(blocks, axis=1)


def _mixer_kernel(x_ref, ln0g_ref, ln0b_ref, wall_ref, convw_ref, convb_ref, dtb_ref,
                  alog_ref, dskip_ref, normg_ref, wpool_ref, bpool_ref, pscale_ref,
                  wout_ref, ln1g_ref, ln1b_ref, wr_ref, br_ref,
                  x1e_ref, cls_ref,
                  xn_s, xnb_s, proj_s, ymix_s, mix_s, hstate):
    s_idx = pl.program_id(1)
    n_blk = TL // SSM_CHUNK

    @pl.when(s_idx == 0)
    def _():
        hstate[...] = jnp.zeros_like(hstate)
        proj_s[0:HALO, :] = jnp.zeros((HALO, C_ALL), F32)

    @pl.when(s_idx != 0)
    def _():
        proj_s[0:HALO, :] = proj_s[TL:TL + HALO, :]

    def ln0_blk(i, carry):
        r0 = pl.multiple_of(i * SSM_CHUNK, SSM_CHUNK)
        xn = _layer_norm(x_ref[pl.ds(r0, SSM_CHUNK), :], ln0g_ref[...], ln0b_ref[...])
        xn_s[pl.ds(r0, SSM_CHUNK), :] = xn
        xnb_s[pl.ds(r0, SSM_CHUNK), :] = xn.astype(BF16)
        return carry
    lax.fori_loop(0, n_blk, ln0_blk, 0)

    proj_s[HALO:HALO + TL, :] = _dot(xnb_s[...], wall_ref[...])

    row_i = lax.broadcasted_iota(jnp.int32, (SSM_CHUNK, LANES), 0)
    lane_i = lax.broadcasted_iota(jnp.int32, (SSM_CHUNK, LANES), 1)
    tril = row_i >= lane_i
    tri = jnp.where(tril, 1.0, 0.0).astype(BF16)
    tri_t = jnp.where(row_i <= lane_i, 1.0, 0.0).astype(BF16)
    lane_lo = lane_i < SSM_HEAD_DIM
    lane_lo1 = lane_lo[0:1, :]
    a_head = -jnp.exp(alog_ref[...])

    def chunk(i, carry):
        r0 = pl.multiple_of(i * SSM_CHUNK, SSM_CHUNK)
        p0 = r0 + HALO

        ext = proj_s[pl.ds(pl.multiple_of(p0 - 8, 8), SSM_CHUNK + 8), C_XBC:C_XBC + D_XBC]
        cw = convw_ref[...]
        acc = ext * cw[3:4, :]
        for k in range(3):
            acc = acc + pltpu.roll(ext, 3 - k, 0) * cw[k:k + 1, :]
        xbc = _silu(acc[8:, :] + convb_ref[...])
        xs = xbc[:, 0:D_SSM]
        bm = xbc[:, D_SSM:D_SSM + 2 * SSM_STATE]
        cm = xbc[:, D_SSM + 2 * SSM_STATE:]

        dt_raw = proj_s[pl.ds(p0, SSM_CHUNK), C_DT:C_DT + LANES] + dtb_ref[...]
        dt = jnp.maximum(dt_raw, 0.0) + jnp.log1p(jnp.exp(-jnp.abs(dt_raw)))
        a = dt * a_head
        a_hi, a_mid, a_lo = _split3(a)
        acum = _dot(tri, a_lo) + _dot(tri, a_mid) + _dot(tri, a_hi)
        dt_t = dt.T
        a_t = a.T[0:16, :]
        t_hi, t_mid, t_lo = _split3(a_t)
        acum_t = _dot(t_lo, tri_t) + _dot(t_mid, tri_t) + _dot(t_hi, tri_t)
        tot = acum[SSM_CHUNK - 1:SSM_CHUNK, :]
        wdec = dt * jnp.exp(tot - acum)
        eac = jnp.exp(acum)

        w_e = _pair_expand(
            lambda h: jnp.broadcast_to(wdec[:, h:h + 1], (SSM_CHUNK, LANES)), lane_lo)
        xw = (xs * w_e).astype(BF16)
        etot = jnp.exp(tot)
        etot_e = _pair_expand(
            lambda h: jnp.broadcast_to(etot[:, h:h + 1], (1, LANES)), lane_lo1)

        ys = []
        for g in range(SSM_GROUPS):
            bg = bm[:, g * SSM_STATE:(g + 1) * SSM_STATE]
            cg = cm[:, g * SSM_STATE:(g + 1) * SSM_STATE]
            bt_b = bg.T.astype(BF16)
            scores = _dot(cg.astype(BF16), bt_b)
            gsl = slice(g * 256, (g + 1) * 256)
            st = _dot(bt_b, xw[:, gsl])
            for kk in range(2):
                k = 2 * g + kk
                psl = slice(k * LANES, (k + 1) * LANES)
                rcat = jnp.concatenate([xs[:, psl], hstate[:, psl]], axis=0)
                y_pair = None
                for hh in range(2):
                    h = 2 * k + hh
                    colb = jnp.broadcast_to(acum[:, h:h + 1], (SSM_CHUNK, LANES))
                    diff = colb - acum_t[h:h + 1, :]
                    decay = jnp.exp(jnp.where(tril, diff, -jnp.inf))
                    m_h = scores * decay * dt_t[h:h + 1, :]
                    c_off = cg * jnp.broadcast_to(eac[:, h:h + 1], (SSM_CHUNK, LANES))
                    lhs = jnp.concatenate([m_h, c_off], axis=1).astype(BF16)
                    keep = lane_lo if hh == 0 else jnp.logical_not(lane_lo)
                    keep2 = jnp.concatenate([keep, keep], axis=0)
                    rhs = jnp.where(keep2, rcat, 0.0).astype(BF16)
                    part = _dot(lhs, rhs)
                    y_pair = part if y_pair is None else y_pair + part
                ys.append(y_pair)
            hstate[:, gsl] = hstate[:, gsl] * etot_e[:, gsl] + st
        y = jnp.concatenate(ys, axis=1)
        y = y + xs * dskip_ref[...]
        z = proj_s[pl.ds(p0, SSM_CHUNK), C_Z:C_Z + D_SSM]
        y = y * _silu(z)
        half = D_SSM // SSM_GROUPS
        outs = []
        for g in range(SSM_GROUPS):
            yg = y[:, g * half:(g + 1) * half]
            ms = jnp.mean(yg * yg, axis=-1, keepdims=True)
            outs.append(yg * lax.rsqrt(ms + RMS_EPS))
        y_ssd = jnp.concatenate(outs, axis=1) * normg_ref[...]
        ymix_s[pl.ds(r0, SSM_CHUNK), 0:D_SSM] = y_ssd.astype(BF16)

        uext = proj_s[pl.ds(p0 - HALO, SSM_CHUNK + HALO), C_U:C_U + D_POOL]
        pos = (s_idx * TL + r0 + 1 + row_i).astype(F32)
        for gi, w in enumerate(POOL_WINDOWS):
            ug = uext[:, gi * POOL_GROUP:(gi + 1) * POOL_GROUP]
            sw = ug
            step = 1
            while step < w:
                sw = sw + pltpu.roll(sw, step, 0)
                step *= 2
            cnt = jnp.minimum(pos, float(w))
            p = sw[HALO:, :] / cnt - ug[HALO:, :]
            mixed = _dot(p.astype(BF16), wpool_ref[gi]) + bpool_ref[:, gi * POOL_GROUP:(gi + 1) * POOL_GROUP]
            mixed = mixed * pscale_ref[:, gi * POOL_GROUP:(gi + 1) * POOL_GROUP]
            ymix_s[pl.ds(r0, SSM_CHUNK), D_SSM + gi * POOL_GROUP:D_SSM + (gi + 1) * POOL_GROUP] = (
                mixed.astype(BF16))
        return carry
    lax.fori_loop(0, n_blk, chunk, 0)

    mix_s[...] = _dot(ymix_s[...], wout_ref[...])

    sub_i = lax.broadcasted_iota(jnp.int32, (8, LANES), 0)

    def tail_blk(i, carry):
        r0 = pl.multiple_of(i * SSM_CHUNK, SSM_CHUNK)
        v = ALPHA * xn_s[pl.ds(r0, SSM_CHUNK), :] + mix_s[pl.ds(r0, SSM_CHUNK), :]
        x1 = _layer_norm(v, ln1g_ref[...], ln1b_ref[...])
        x1e_ref[pl.ds(r0, SSM_CHUNK), 0:D_MODEL] = x1
        logits = _dot(x1.astype(BF16), wr_ref[...]) + br_ref[...]
        lt = logits.T
        gl = [lt[R_GROUP + k:R_GROUP + k + 1, :] for k in range(N_EXPERT_GROUPS)]
        gmax = jnp.maximum(jnp.maximum(gl[0], gl[1]), jnp.maximum(gl[2], gl[3]))
        ge = [jnp.exp(v_ - gmax) for v_ in gl]
        gsum = ge[0] + ge[1] + ge[2] + ge[3]
        gp = [e_ / gsum for e_ in ge]
        pmax = jnp.maximum(jnp.maximum(gp[0], gp[1]), jnp.maximum(gp[2], gp[3]))
        gidx = jnp.where(gp[0] == pmax, 0, jnp.where(gp[1] == pmax, 1, jnp.where(gp[2] == pmax, 2, 3)))
        es = []
        for j in range(EXPERTS_PER_GROUP):
            rows = [lt[R_EXPERT + 4 * k + j:R_EXPERT + 4 * k + j + 1, :] for k in range(N_EXPERT_GROUPS)]
            es.append(jnp.where(gidx == 0, rows[0], jnp.where(gidx == 1, rows[1],
                      jnp.where(gidx == 2, rows[2], rows[3]))))
        v1 = jnp.maximum(jnp.maximum(es[0], es[1]), jnp.maximum(es[2], es[3]))
        i1 = jnp.where(es[0] == v1, 0, jnp.where(es[1] == v1, 1, jnp.where(es[2] == v1, 2, 3)))
        em = [jnp.where(i1 == j, -jnp.inf, es[j]) for j in range(EXPERTS_PER_GROUP)]
        v2 = jnp.maximum(jnp.maximum(em[0], em[1]), jnp.maximum(em[2], em[3]))
        i2 = jnp.where(em[0] == v2, 0, jnp.where(em[1] == v2, 1, jnp.where(em[2] == v2, 2, 3)))
        e2 = jnp.exp(v2 - v1)
        den = 1.0 + e2
        c1 = pmax * (1.0 / den)
        c2 = pmax * (e2 / den)
        first_low = i1 < i2
        ia = jnp.where(first_low, i1, i2)
        ib = jnp.where(first_low, i2, i1)
        wa = jnp.where(first_low, c1, c2)
        wb = jnp.where(first_low, c2, c1)
        pid = jnp.where(ia == 0, ib - 1, jnp.where(ia == 1, ib + 1, 5))
        cls = gidx * N_PAIRS + pid
        cls_ref[:, pl.ds(r0, SSM_CHUNK)] = jnp.where(sub_i == 0, cls, 0)
        wrows = jnp.where(row_i == 0, wa, jnp.where(row_i == 1, wb, 0.0))
        x1e_ref[pl.ds(r0, SSM_CHUNK), D_MODEL:X1E_W] = wrows.T
        return carry
    lax.fori_loop(0, n_blk, tail_blk, 0)


def _mixer_call(x, ln0_g, ln0_b, w_all, conv_w, conv_b, dtb, alog, dskip, normg, w_pool,
                b_pool, pscale, w_out, ln1_g, ln1_b, w_r, b_r):
    bsz, seq, _ = x.shape
    n_tok = bsz * seq
    n_s = seq // TL

    def full(shape):
        return pl.BlockSpec(shape, lambda b, s: (0,) * len(shape))

    in_specs = [
        pl.BlockSpec((None, TL, D_MODEL), lambda b, s: (b, s, 0)),
        full((1, D_MODEL)), full((1, D_MODEL)),
        full((D_MODEL, C_ALL)),
        full((SSM_CONV, D_XBC)), full((1, D_XBC)),
        full((1, LANES)), full((1, LANES)),
        full((1, D_SSM)), full((1, D_SSM)),
        full((len(POOL_WINDOWS), POOL_GROUP, POOL_GROUP)), full((1, D_POOL)), full((1, D_POOL)),
        full((D_MODEL, D_MODEL)),
        full((1, D_MODEL)), full((1, D_MODEL)),
        full((D_MODEL, LANES)), full((1, LANES)),
    ]
    out_specs = [
        pl.BlockSpec((TL, X1E_W), lambda b, s: (b * n_s + s, 0)),
        pl.BlockSpec((8, TL), lambda b, s: (0, b * n_s + s)),
    ]
    out_shape = [
        jax.ShapeDtypeStruct((n_tok, X1E_W), F32),
        jax.ShapeDtypeStruct((8, n_tok), jnp.int32),
    ]
    scratch = [
        pltpu.VMEM((TL, D_MODEL), F32),
        pltpu.VMEM((TL, D_MODEL), BF16),
        pltpu.VMEM((TL + HALO, C_ALL), F32),
        pltpu.VMEM((TL, D_MODEL), BF16),
        pltpu.VMEM((TL, D_MODEL), F32),
        pltpu.VMEM((SSM_STATE, D_SSM), F32),
    ]
    return pl.pallas_call(
        _mixer_kernel,
        grid=(bsz, n_s),
        in_specs=in_specs,
        out_specs=out_specs,
        out_shape=out_shape,
        scratch_shapes=scratch,
        compiler_params=pltpu.CompilerParams(
            dimension_semantics=("arbitrary", "arbitrary"),
            vmem_limit_bytes=56 * 1024 * 1024),
        name="token_mixer",
    )(x, ln0_g, ln0_b, w_all, conv_w, conv_b, dtb, alog, dskip, normg, w_pool, b_pool,
      pscale, w_out, ln1_g, ln1_b, w_r, b_r)


def _moe_kernel(sorted_ref, ea_ref, eb_ref, start_ref, nvalid_ref,
                x1e_hbm, wga_ref, wua_ref, wda_ref, wgb_ref, wub_ref, wdb_ref,
                ln2g_ref, ln2b_ref,
                out_hbm,
                xbuf0, xbuf1, obuf0, obuf1, gsem, ssem):
    t = pl.program_id(0)
    parity = lax.rem(t, 2)
    nv = nvalid_ref[t]
    nv_m1 = jnp.where(t >= 1, nvalid_ref[jnp.maximum(t - 1, 0)], 0)
    nv_m2 = jnp.where(t >= 2, nvalid_ref[jnp.maximum(t - 2, 0)], 0)
    xbufs = (xbuf0, xbuf1)
    obufs = (obuf0, obuf1)

    def gather_copy(tok, p, r):
        return pltpu.make_async_copy(x1e_hbm.at[pl.ds(tok, 1), :],
                                     xbufs[p].at[pl.ds(r, 1), :], gsem.at[p])

    def scatter_copy(tok, p, r):
        return pltpu.make_async_copy(obufs[p].at[pl.ds(r, 1), :],
                                     out_hbm.at[pl.ds(tok, 1), :], ssem.at[p])

    def issue_gathers(tile, p):
        base = start_ref[tile]
        for r in range(TM):
            gather_copy(sorted_ref[base + r], p, r).start()

    def wait_gathers(p):
        pltpu.make_async_copy(x1e_hbm.at[pl.ds(0, TM), :], xbufs[p], gsem.at[p]).wait()

    def wait_scatters(p, count):
        @pl.when(count == TM)
        def _():
            pltpu.make_async_copy(obufs[p], out_hbm.at[pl.ds(0, TM), :], ssem.at[p]).wait()

        @pl.when(jnp.logical_and(count > 0, count < TM))
        def _():
            def body(r, carry):
                scatter_copy(0, p, r).wait()
                return carry
            lax.fori_loop(0, count, body, 0)

    def compute(p):
        xbuf = xbufs[p]
        x1 = xbuf[:, 0:D_MODEL]
        xb = x1.astype(BF16)
        wa = xbuf[:, D_MODEL:D_MODEL + 1]
        wb = xbuf[:, D_MODEL + 1:D_MODEL + 2]
        ha = _silu(_dot(xb, wga_ref[...])) * _dot(xb, wua_ref[...]) * wa
        hb = _silu(_dot(xb, wgb_ref[...])) * _dot(xb, wub_ref[...]) * wb
        y = _dot(ha.astype(BF16), wda_ref[...]) + _dot(hb.astype(BF16), wdb_ref[...])
        obufs[p][...] = _layer_norm(ALPHA * x1 + y, ln2g_ref[...], ln2b_ref[...])

    @pl.when(t == 0)
    def _():
        issue_gathers(0, 0)

    def step(p):
        q = 1 - p

        @pl.when(nv > 0)
        def _():
            wait_gathers(p)
            wait_scatters(p, nv_m2)
            issue_gathers(t + 1, q)
            base_m1 = start_ref[jnp.maximum(t - 1, 0)]
            for r in range(TM):
                @pl.when(r < nv_m1)
                def _():
                    scatter_copy(sorted_ref[base_m1 + r], q, r).start()
            compute(p)

        @pl.when(jnp.logical_and(nv == 0, nv_m1 > 0))
        def _():
            wait_gathers(p)
            wait_scatters(p, nv_m2)
            base_m1 = start_ref[jnp.maximum(t - 1, 0)]

            def body(r, carry):
                scatter_copy(sorted_ref[base_m1 + r], q, r).start()
                return carry
            lax.fori_loop(0, nv_m1, body, 0)
            wait_scatters(q, nv_m1)

    for p in range(2):
        pl.when(parity == p)(functools.partial(step, p))


def _moe_call(sorted_tok, tile_ea, tile_eb, tile_start, tile_nvalid, x1e, wg, wu, wd, ln2_g, ln2_b):
    n_tok = x1e.shape[0]
    n_tiles = tile_ea.shape[0]

    def wspec(shape, which):
        if which == 0:
            return pl.BlockSpec((None,) + shape, lambda i, st, ea, eb, s0, nv: (ea[i], 0, 0))
        return pl.BlockSpec((None,) + shape, lambda i, st, ea, eb, s0, nv: (eb[i], 0, 0))

    vec = pl.BlockSpec((1, D_MODEL), lambda i, st, ea, eb, s0, nv: (0, 0))
    grid_spec = pltpu.PrefetchScalarGridSpec(
        num_scalar_prefetch=5,
        grid=(n_tiles,),
        in_specs=[
            pl.BlockSpec(memory_space=pl.ANY),
            wspec((D_MODEL, D_FF), 0), wspec((D_MODEL, D_FF), 0), wspec((D_FF, D_MODEL), 0),
            wspec((D_MODEL, D_FF), 1), wspec((D_MODEL, D_FF), 1), wspec((D_FF, D_MODEL), 1),
            vec, vec,
        ],
        out_specs=pl.BlockSpec(memory_space=pl.ANY),
        scratch_shapes=[
            pltpu.VMEM((TM, X1E_W), F32), pltpu.VMEM((TM, X1E_W), F32),
            pltpu.VMEM((TM, D_MODEL), F32), pltpu.VMEM((TM, D_MODEL), F32),
            pltpu.SemaphoreType.DMA((2,)),
            pltpu.SemaphoreType.DMA((2,)),
        ],
    )
    return pl.pallas_call(
        _moe_kernel,
        grid_spec=grid_spec,
        out_shape=jax.ShapeDtypeStruct((n_tok, D_MODEL), F32),
        compiler_params=pltpu.CompilerParams(
            dimension_semantics=("arbitrary",),
            vmem_limit_bytes=48 * 1024 * 1024),
        name="sparse_moe",
    )(sorted_tok, tile_ea, tile_eb, tile_start, tile_nvalid, x1e, wg, wu, wd, wg, wu, wd,
      ln2_g, ln2_b)


def _dispatch_plan(cls, n_tok):
    n_tiles = n_tok // TM + N_CLASSES
    tok = jnp.arange(n_tok, dtype=jnp.int32)
    key = jnp.sort(cls * n_tok + tok)
    sorted_tok = key % n_tok
    counts = jnp.sum((cls[:, None] == jnp.arange(N_CLASSES, dtype=jnp.int32)[None, :]).astype(jnp.int32), axis=0)
    class_start = jnp.cumsum(counts) - counts
    tiles_per = (counts + TM - 1) // TM
    tile_end = jnp.cumsum(tiles_per)
    tile_first = tile_end - tiles_per
    total = tile_end[-1]
    ti = jnp.arange(n_tiles, dtype=jnp.int32)
    tc = jnp.sum((ti[:, None] >= tile_end[None, :]).astype(jnp.int32), axis=1)
    last_c = jnp.sum((total - 1 >= tile_end).astype(jnp.int32))
    tc = jnp.where(ti < total, tc, last_c)
    k = ti - tile_first[tc]
    start = class_start[tc] + k * TM
    nvalid = jnp.where(ti < total, jnp.clip(counts[tc] - k * TM, 0, TM), 0)
    start = jnp.where(ti < total, start, 0)
    grp = tc // N_PAIRS
    pid = tc % N_PAIRS
    pa = jnp.asarray(PAIR_A, jnp.int32)
    pb = jnp.asarray(PAIR_B, jnp.int32)
    ea = grp * EXPERTS_PER_GROUP + pa[pid]
    eb = grp * EXPERTS_PER_GROUP + pb[pid]
    sorted_tok = jnp.pad(sorted_tok, (0, TM))
    return (sorted_tok.astype(jnp.int32), ea.astype(jnp.int32), eb.astype(jnp.int32),
            start.astype(jnp.int32), nvalid.astype(jnp.int32))


def kernel(x, ln0_g, ln0_b, w_in, conv_w, conv_b, dt_bias, a_log, d_skip, ssm_norm_g, w_pool,
           b_pool, pool_scale, w_out, ln1_g, ln1_b, w_router_group, b_router_group,
           w_router_expert, b_router_expert, w_gate, w_up, w_down, ln2_g, ln2_b):
    bsz, seq, d = x.shape
    n_tok = bsz * seq
    layer = 0
    s1, s2, s3 = D_SSM, D_SSM + D_XBC, D_SSM + D_XBC + SSM_HEADS
    wi = w_in[layer]
    w_dt = jnp.pad(wi[:, s2:s3], ((0, 0), (0, LANES - SSM_HEADS)))
    w_all = jnp.concatenate([wi[:, :s2], wi[:, s3:], w_dt], axis=1).astype(BF16)
    dtb = jnp.pad(dt_bias[layer], (0, LANES - SSM_HEADS))[None, :]
    alog = jnp.pad(a_log[layer], (0, LANES - SSM_HEADS))[None, :]
    dskip = jnp.repeat(d_skip[layer], SSM_HEAD_DIM)[None, :]
    w_r = jnp.zeros((d, LANES), F32)
    w_r = w_r.at[:, R_GROUP:R_GROUP + N_EXPERT_GROUPS].set(w_router_group[layer])
    w_r = w_r.at[:, R_EXPERT:R_EXPERT + N_EXPERTS].set(w_router_expert[layer]).astype(BF16)
    b_r = jnp.zeros((LANES,), F32)
    b_r = b_r.at[R_GROUP:R_GROUP + N_EXPERT_GROUPS].set(b_router_group[layer])
    b_r = b_r.at[R_EXPERT:R_EXPERT + N_EXPERTS].set(b_router_expert[layer])[None, :]

    x1e, cls8 = _mixer_call(
        x, ln0_g[None, :], ln0_b[None, :], w_all, conv_w[layer], conv_b[layer][None, :], dtb, alog,
        dskip, ssm_norm_g[layer][None, :], w_pool[layer].astype(BF16),
        b_pool[layer].reshape(1, D_POOL), pool_scale[layer][None, :], w_out[layer].astype(BF16),
        ln1_g[layer][None, :], ln1_b[layer][None, :], w_r, b_r)

    plan = _dispatch_plan(cls8[0], n_tok)
    out = _moe_call(*plan, x1e, w_gate[layer].astype(BF16), w_up[layer].astype(BF16),
                    w_down[layer].astype(BF16), ln2_g[layer][None, :], ln2_b[layer][None, :])
    return out.reshape(bsz, seq, d)
```
